```python
import jax, jax.numpy as jnp
from jax import lax
import numpy as np

D_MODEL = 1024
BATCH = 1
SEQ = 16384
DEPTH = 2

HG_HEADS = 4
HG_DK = 128
HG_DV = 128
HG_WIDTH = HG_HEADS * HG_DV
SA_HEADS = 4
SA_DH = 64
SA_WIDTH = SA_HEADS * SA_DH
MEM_HEADS = 4
MEM_DH = 64
MEM_WIDTH = MEM_HEADS * MEM_DH
MIX_WIDTH = HG_WIDTH + SA_WIDTH + MEM_WIDTH
IDX_HEADS = 8
IDX_DH = 64
TOPK_MAX = 256
N_MEM = 256
D_FF = -(-8 * D_MODEL // (3 * 256)) * 256
CHUNK = 64
Q_BLOCK = 128
EPS = 1e-6
IN_SIZES = (HG_HEADS * HG_DK, HG_HEADS * HG_DK, HG_WIDTH, HG_WIDTH,
            SA_WIDTH, SA_WIDTH, SA_WIDTH,
            IDX_HEADS * IDX_DH, IDX_DH, IDX_HEADS,
            MEM_WIDTH)
IN_WIDTH = sum(IN_SIZES)

kernel_name = "hgrn2_dsa_memory_hybrid_block"

f32 = jnp.float32


def rmsnorm(x, gain):
    x32 = x.astype(f32)
    y = x32 * lax.rsqrt(jnp.mean(x32 * x32, axis=-1, keepdims=True) + EPS)
    return (y * gain.astype(f32)).astype(x.dtype)


def split_in(p):
    offs, acc = [], 0
    for s in IN_SIZES[:-1]:
        acc += s
        offs.append(acc)
    return jnp.split(p, offs, axis=-1)


def to_chunks(a, B, N):
    return a.reshape(B, N, CHUNK, a.shape[2], a.shape[3]).transpose(1, 0, 3, 2, 4)


def hgrn2_mix(q, f_logit, i, g, lb, out_gain):
    B, L = q.shape[:2]
    N = L // CHUNK
    lb = lb.reshape(HG_HEADS, HG_DK)
    z = f_logit.astype(f32)
    log_f = jnp.logaddexp(jnp.log(lb), jnp.log1p(-lb) + jax.nn.log_sigmoid(z))
    k = (1.0 - lb) * jax.nn.sigmoid(-z)
    qf = jax.nn.silu(q.astype(f32))
    xs = (to_chunks(qf, B, N), to_chunks(k, B, N),
          to_chunks(i.astype(f32), B, N), to_chunks(log_f, B, N))
    causal = jnp.tril(jnp.ones((CHUNK, CHUNK), dtype=bool))

    def step(S, inp):
        qc, kc, vc, gc = inp
        b = jnp.cumsum(gc, axis=2)
        o_inter = jnp.einsum('bhck,bhkv->bhcv', qc * jnp.exp(b), S)
        diff = b[:, :, :, None, :] - b[:, :, None, :, :]
        decay = jnp.exp(jnp.where(causal[:, :, None], diff, -jnp.inf))
        A = jnp.einsum('bhtk,bhtsk,bhsk->bhts', qc, decay, kc)
        o = o_inter + jnp.einsum('bhts,bhsv->bhtv', A, vc)
        b_last = b[:, :, -1:, :]
        S = jnp.exp(b_last[:, :, 0, :])[..., None] * S + \
            jnp.einsum('bhck,bhcv->bhkv', kc * jnp.exp(b_last - b), vc)
        return S, o

    S0 = jnp.zeros((B, HG_HEADS, HG_DK, HG_DV), f32)
    _, o = lax.scan(step, S0, xs)
    o = o.transpose(1, 0, 3, 2, 4).reshape(B, L, HG_HEADS, HG_DV)
    o = rmsnorm(o, out_gain) * jax.nn.silu(g.astype(f32))
    return o.reshape(B, L, HG_WIDTH).astype(q.dtype)


def dsa_mix(q, k, v, iq, ik, iw, q_gain, k_gain):
    B, L = q.shape[:2]
    topk = min(TOPK_MAX, L // 4)
    nb = L // Q_BLOCK
    q = rmsnorm(q, q_gain).astype(f32)
    k = rmsnorm(k, k_gain).astype(f32)
    v32 = v.astype(f32)
    ik32 = ik.astype(f32)
    iw32 = iw.astype(f32) * (IDX_HEADS ** -0.5 * IDX_DH ** -0.5)
    key_pos = jnp.arange(L)
    bidx = jnp.arange(B)[:, None, None]

    def blockify(a):
        return a.reshape(B, nb, Q_BLOCK, *a.shape[2:]).swapaxes(0, 1)

    def one_block(args):
        blk, qb, iqb, iwb = args
        qpos = blk * Q_BLOCK + jnp.arange(Q_BLOCK)
        s = jnp.einsum('bqhd,bsd->bqhs', iqb.astype(f32), ik32)
        score = jnp.einsum('bqh,bqhs->bqs', iwb, jax.nn.relu(s))
        visible = key_pos[None, :] <= qpos[:, None]
        score = jnp.where(visible[None], score, -jnp.inf)
        _, idx = lax.top_k(score, topk)
        valid = idx <= qpos[None, :, None]
        kg = k[bidx, idx]
        vg = v32[bidx, idx]
        logits = jnp.einsum('bqhd,bqkhd->bhqk', qb, kg) * (SA_DH ** -0.5)
        logits = jnp.where(valid[:, None], logits, -jnp.inf)
        p = jax.nn.softmax(logits, axis=-1)
        return jnp.einsum('bhqk,bqkhd->bqhd', p, vg)

    out = lax.map(one_block, (jnp.arange(nb), blockify(q), blockify(iq), blockify(iw32)))
    return out.swapaxes(0, 1).reshape(B, L, SA_WIDTH).astype(v.dtype)


def mem_mix(qm, mem_n, w_kv, q_gain, k_gain):
    B, L = qm.shape[:2]
    km, vm = jnp.split(mem_n @ w_kv, 2, axis=-1)
    M = mem_n.shape[1]
    km = km.reshape(B, M, MEM_HEADS, MEM_DH)
    vm = vm.reshape(B, M, MEM_HEADS, MEM_DH)
    qm = rmsnorm(qm, q_gain).astype(f32)
    km = rmsnorm(km, k_gain).astype(f32)
    logits = jnp.einsum('bqhd,bmhd->bhqm', qm, km) * (MEM_DH ** -0.5)
    p = jax.nn.softmax(logits, axis=-1)
    out = jnp.einsum('bhqm,bmhd->bqhd', p, vm.astype(f32))
    return out.reshape(B, L, MEM_WIDTH).astype(mem_n.dtype)


def setup_inputs(seed: int = 0) -> dict:
    key = jax.random.key(seed)
    ks = jax.random.split(key, 16)
    nrm = jax.random.normal

    def gain(k, n):
        return 1.0 + 0.02 * nrm(k, (DEPTH, n), f32)

    return {
        "x": nrm(ks[0], (BATCH, SEQ, D_MODEL), f32),
        "mem": nrm(ks[1], (BATCH, N_MEM, D_MODEL), f32),
        "w_in": nrm(ks[2], (DEPTH, D_MODEL, IN_WIDTH), f32) * D_MODEL ** -0.5,
        "w_out": nrm(ks[3], (DEPTH, MIX_WIDTH, D_MODEL), f32) * MIX_WIDTH ** -0.5,
        "w_mem_kv": nrm(ks[4], (DEPTH, D_MODEL, 2 * MEM_WIDTH), f32) * D_MODEL ** -0.5,
        "lb_logits": nrm(ks[5], (DEPTH, HG_HEADS * HG_DK), f32),
        "norm_mix": gain(ks[6], D_MODEL),
        "norm_mem": gain(ks[7], D_MODEL),
        "norm_ffn": gain(ks[8], D_MODEL),
        "hg_out_gain": gain(ks[9], HG_DV),
        "sa_q_gain": gain(ks[10], SA_DH),
        "sa_k_gain": gain(ks[11], SA_DH),
        "mem_q_gain": gain(ks[12], MEM_DH),
        "mem_k_gain": gain(ks[13], MEM_DH),
        "w_ffn_in": nrm(ks[14], (DEPTH, D_MODEL, 2 * D_FF), f32) * D_MODEL ** -0.5,
        "w_ffn_out": nrm(ks[15], (DEPTH, D_FF, D_MODEL), f32) * D_FF ** -0.5,
    }


def reference(x, mem, w_in, w_out, w_mem_kv, lb_logits, norm_mix, norm_mem, norm_ffn,
              hg_out_gain, sa_q_gain, sa_k_gain, mem_q_gain, mem_k_gain,
              w_ffn_in, w_ffn_out):
    B, L, _ = x.shape
    lb = jnp.cumsum(jax.nn.softmax(lb_logits.astype(f32), axis=0), axis=0)
    lb = lb - lb[0:1]
    for layer in range(DEPTH):
        h = rmsnorm(x, norm_mix[layer])
        hq, hf, hi, hg, sq, sk, sv, iq, ik, iw, mq = split_in(h @ w_in[layer])
        o_hg = hgrn2_mix(hq.reshape(B, L, HG_HEADS, HG_DK), hf.reshape(B, L, HG_HEADS, HG_DK),
                         hi.reshape(B, L, HG_HEADS, HG_DV), hg.reshape(B, L, HG_HEADS, HG_DV),
                         lb[layer], hg_out_gain[layer])
        o_sa = dsa_mix(sq.reshape(B, L, SA_HEADS, SA_DH), sk.reshape(B, L, SA_HEADS, SA_DH),
                       sv.reshape(B, L, SA_HEADS, SA_DH), iq.reshape(B, L, IDX_HEADS, IDX_DH),
                       ik, iw, sa_q_gain[layer], sa_k_gain[layer])
        mem_n = rmsnorm(mem, norm_mem[layer])
        o_mem = mem_mix(mq.reshape(B, L, MEM_HEADS, MEM_DH), mem_n, w_mem_kv[layer],
                        mem_q_gain[layer], mem_k_gain[layer])
        mixed = jnp.concatenate([o_hg, o_sa, o_mem], axis=-1)
        x = x + (mixed @ w_out[layer]).astype(x.dtype)
        h = rmsnorm(x, norm_ffn[layer])
        a, b = jnp.split(h @ w_ffn_in[layer], 2, axis=-1)
        x = x + ((jax.nn.silu(a) * b) @ w_ffn_out[layer]).astype(x.dtype)
    return x
```

```python
import functools

import numpy as np
import jax
import jax.numpy as jnp
from jax import lax
from jax.experimental import pallas as pl
from jax.experimental.pallas import tpu as pltpu

f32 = jnp.float32
bf16 = jnp.bfloat16
i32 = jnp.int32

D_MODEL = 1024
HG_HEADS = 4
HG_DK = 128
HG_DV = 128
HG_WIDTH = HG_HEADS * HG_DV
SA_HEADS = 4
SA_DH = 64
SA_WIDTH = SA_HEADS * SA_DH
MEM_HEADS = 4
MEM_DH = 64
MEM_WIDTH = MEM_HEADS * MEM_DH
IDX_HEADS = 8
IDX_DH = 64
IDX_WIDTH = IDX_HEADS * IDX_DH
TOPK_MAX = 256
D_FF = 2816
EPS = 1e-6

LANES = 128
HG_IN = 4 * HG_WIDTH
IN_MAIN = HG_IN + 3 * SA_WIDTH + IDX_WIDTH + MEM_WIDTH
IN_PAD = IN_MAIN + LANES
IW_LANE0 = IDX_DH

HG_CHUNK = 128
HG_LEVELS = 7
Q_ROWS = 128
KEY_CHUNK = 512
IN_ROWS = 512
FFN_ROWS = 256

INT_MIN = -2 ** 31
NEG_BIG = -1e30
VMEM_LIMIT = 56 * 1024 * 1024


def _dot(a, b):
    return jnp.dot(a, b, preferred_element_type=f32)


def _dot_nt(a, b):
    return lax.dot_general(a, b, (((1,), (1,)), ((), ())), preferred_element_type=f32)


def _dot_tn(a, b):
    return lax.dot_general(a, b, (((0,), (0,)), ((), ())), preferred_element_type=f32)


def _split3(x):
    a = x.astype(bf16)
    r = x - a.astype(f32)
    b = r.astype(bf16)
    c = (r - b.astype(f32)).astype(bf16)
    return a, b, c


def _dot_sel_right(x, m01):
    a, b, c = _split3(x)
    return _dot(a, m01) + _dot(b, m01) + _dot(c, m01)


def _dot_sel_left(m01, x):
    a, b, c = _split3(x)
    return _dot(m01, a) + _dot(m01, b) + _dot(m01, c)


def _rmsnorm_rows(x, gain):
    ms = jnp.mean(x * x, axis=-1, keepdims=True)
    return x * lax.rsqrt(ms + EPS) * gain


def _headnorm(t, gain, grp, width):
    ss = _dot_sel_right(t * t, grp)
    return t * lax.rsqrt(ss * (1.0 / width) + EPS) * gain


def _resident(shape):
    zeros = (0,) * len(shape)
    return pl.BlockSpec(shape, lambda *_: zeros, pipeline_mode=pl.Buffered(1))


def _group_matrix(n, width):
    g = np.arange(n) // width
    return (g[:, None] == g[None, :]).astype(np.float32)


def _iw_expand(width):
    m = np.zeros((LANES, IDX_HEADS * width), np.float32)
    for h in range(IDX_HEADS):
        m[IW_LANE0 + h, h * width:(h + 1) * width] = 1.0
    return m


def _strict_upper(n):
    a = np.arange(n)
    return (a[:, None] < a[None, :]).astype(np.float32)


def _hgrn_constants():
    n = HG_CHUNK
    t = np.arange(n)[:, None]
    r = np.arange(n)[None, :]
    mats = []
    for lv in range(2, HG_LEVELS + 2):
        h = 1 << (lv - 1)
        mats.append(((r >= (t // h) * h) & (r <= t)).astype(np.float32))
        mats.append(((r > t) & (r <= (t // h) * h + h - 1)).astype(np.float32))
    stack = np.concatenate(mats, axis=0)
    x = t ^ r
    lvl = np.where(x == 0, 0, np.floor(np.log2(np.maximum(x, 1))).astype(np.int64) + 1)
    lvl = np.where(r > t, -1, lvl).astype(np.int32)
    return stack, lvl


def _memkv_kernel(mem_ref, g_ref, w_ref, gk_ref, grp_ref, kmT_ref, vm_ref):
    mn = _rmsnorm_rows(mem_ref[...], g_ref[...]).astype(bf16)
    kv = _dot(mn, w_ref[...])
    km = _headnorm(kv[:, :MEM_WIDTH], gk_ref[...], grp_ref[...], MEM_DH)
    kmT_ref[...] = km.T.astype(bf16)
    vm_ref[...] = kv[:, MEM_WIDTH:].astype(bf16)


def _memkv(mem2d, g_mem, w_kv, gk, grp):
    m = mem2d.shape[0]
    return pl.pallas_call(
        _memkv_kernel,
        out_shape=(jax.ShapeDtypeStruct((MEM_WIDTH, m), bf16),
                   jax.ShapeDtypeStruct((m, MEM_WIDTH), bf16)),
        name="memkv",
    )(mem2d, g_mem, w_kv, gk, grp)


def _inproj_kernel(x_ref, g_ref, w_ref, gq_ref, gk_ref, gm_ref, grp_ref, exp_ref,
                   hgp_ref, qn_ref, knT_ref, v_ref, iqs_ref, ikT_ref, sg_ref, mqn_ref):
    h = _rmsnorm_rows(x_ref[...], g_ref[...]).astype(bf16)
    p = _dot(h, w_ref[...])
    hgp_ref[...] = p[:, :HG_IN]
    o = HG_IN
    sq = p[:, o:o + SA_WIDTH]
    sk = p[:, o + SA_WIDTH:o + 2 * SA_WIDTH]
    sv = p[:, o + 2 * SA_WIDTH:o + 3 * SA_WIDTH]
    o += 3 * SA_WIDTH
    iq = p[:, o:o + IDX_WIDTH]
    o += IDX_WIDTH
    mq = p[:, o:o + MEM_WIDTH]
    last = p[:, IN_MAIN:IN_PAD]
    grp = grp_ref[...]
    qn_ref[...] = (_headnorm(sq, gq_ref[...], grp, SA_DH) * (SA_DH ** -0.5)).astype(bf16)
    knT_ref[...] = _headnorm(sk, gk_ref[...], grp, SA_DH).T.astype(bf16)
    v_ref[...] = sv.astype(bf16)
    wabs = _dot_sel_right(jnp.abs(last), exp_ref[...]) * (IDX_HEADS ** -0.5 * IDX_DH ** -0.5)
    iqs_ref[...] = (iq * wabs).astype(bf16)
    ikT_ref[...] = last.T[:IDX_DH, :].astype(bf16)
    sg_ref[...] = jnp.where(last >= 0.0, 1.0, -1.0).astype(bf16)
    mqn_ref[...] = (_headnorm(mq, gm_ref[...], grp, MEM_DH) * (MEM_DH ** -0.5)).astype(bf16)


def _inproj(x2d, g_mix, w_in, gq, gk, gm, grp, expm):
    L = x2d.shape[0]
    tm = IN_ROWS
    rows = lambda w: pl.BlockSpec((tm, w), lambda i: (i, 0))
    cols = lambda h: pl.BlockSpec((h, tm), lambda i: (0, i))
    return pl.pallas_call(
        _inproj_kernel,
        grid=(L // tm,),
        in_specs=[rows(D_MODEL), _resident((1, D_MODEL)), _resident((D_MODEL, IN_PAD)),
                  _resident((1, SA_WIDTH)), _resident((1, SA_WIDTH)), _resident((1, MEM_WIDTH)),
                  _resident((SA_WIDTH, SA_WIDTH)), _resident((LANES, IDX_WIDTH))],
        out_specs=(rows(HG_IN), rows(SA_WIDTH), cols(SA_WIDTH), rows(SA_WIDTH),
                   rows(IDX_WIDTH), cols(IDX_DH), rows(LANES), rows(MEM_WIDTH)),
        out_shape=(jax.ShapeDtypeStruct((L, HG_IN), f32),
                   jax.ShapeDtypeStruct((L, SA_WIDTH), bf16),
                   jax.ShapeDtypeStruct((SA_WIDTH, L), bf16),
                   jax.ShapeDtypeStruct((L, SA_WIDTH), bf16),
                   jax.ShapeDtypeStruct((L, IDX_WIDTH), bf16),
                   jax.ShapeDtypeStruct((IDX_DH, L), bf16),
                   jax.ShapeDtypeStruct((L, LANES), bf16),
                   jax.ShapeDtypeStruct((L, MEM_WIDTH), bf16)),
        compiler_params=pltpu.CompilerParams(dimension_semantics=("arbitrary",),
                                             vmem_limit_bytes=VMEM_LIMIT),
        name="inproj",
    )(x2d, g_mix, w_in, gq, gk, gm, grp, expm)


def _hgrn_kernel(layer, q_ref, f_ref, i_ref, g_ref, lbl_ref, og_ref, ms_ref, lvl_ref,
                 o_ref, st_ref):
    @pl.when(pl.program_id(1) == 0)
    def _():
        st_ref[...] = jnp.zeros_like(st_ref)

    lg = lbl_ref[...]
    e = jnp.exp(lg - jnp.max(lg, axis=0, keepdims=True))
    sm = e / jnp.sum(e, axis=0, keepdims=True)
    lb = jnp.zeros((1, HG_DK), f32)
    for j in range(1, layer + 1):
        lb = lb + sm[j:j + 1, :]

    z = f_ref[...]
    log_sig = jnp.minimum(z, 0.0) - jnp.log1p(jnp.exp(-jnp.abs(z)))
    c = jnp.log1p(-lb) + log_sig
    a = jnp.log(lb)
    lf = jnp.maximum(a, c) + jnp.log1p(jnp.exp(-jnp.abs(a - c)))
    kk = (1.0 - lb) * jax.nn.sigmoid(-z)
    q = q_ref[...]
    qf = q * jax.nn.sigmoid(q)
    v = i_ref[...].astype(bf16)

    n = HG_CHUNK
    ex = _dot_sel_left(ms_ref[...], lf)
    lvl = lvl_ref[...]
    kb = kk.astype(bf16)
    att = jnp.where(lvl == 0, _dot_nt(qf.astype(bf16), kb), 0.0)
    att = jnp.where(lvl == 1, _dot_nt((qf * jnp.exp(lf)).astype(bf16), kb), att)
    for lv in range(2, HG_LEVELS + 1):
        eq = ex[(2 * lv - 4) * n:(2 * lv - 3) * n]
        ek = ex[(2 * lv - 3) * n:(2 * lv - 2) * n]
        part = _dot_nt((qf * jnp.exp(eq)).astype(bf16), (kk * jnp.exp(ek)).astype(bf16))
        att = jnp.where(lvl == lv, part, att)
    top = 2 * HG_LEVELS - 2
    eb = ex[top * n:(top + 1) * n]
    ekd = ex[(top + 1) * n:(top + 2) * n]
    st = st_ref[...]
    o = _dot_nt((qf * jnp.exp(eb)).astype(bf16), st.astype(bf16)) + _dot(att.astype(bf16), v)
    st_ref[...] = st * jnp.exp(eb[n - 1:n, :]) + _dot_tn(v, (kk * jnp.exp(ekd)).astype(bf16))

    gate = g_ref[...]
    o_ref[...] = (_rmsnorm_rows(o, og_ref[...]) * (gate * jax.nn.sigmoid(gate))).astype(bf16)


def _hgrn(layer, hgp, lb_logits, out_gain, mstack, lvl):
    L = hgp.shape[0]
    depth = lb_logits.shape[0]
    n = HG_CHUNK
    col = lambda g: pl.BlockSpec((n, HG_DK), lambda h, t, g=g: (t, g * HG_HEADS + h))
    return pl.pallas_call(
        functools.partial(_hgrn_kernel, layer),
        grid=(HG_HEADS, L // n),
        in_specs=[col(0), col(1), col(2), col(3),
                  pl.BlockSpec((depth, HG_DK), lambda h, t: (0, h)),
                  _resident((1, HG_DV)), _resident(mstack.shape), _resident((n, n))],
        out_specs=pl.BlockSpec((n, HG_DV), lambda h, t: (t, h)),
        out_shape=jax.ShapeDtypeStruct((L, HG_WIDTH), bf16),
        scratch_shapes=[pltpu.VMEM((HG_DV, HG_DK), f32)],
        compiler_params=pltpu.CompilerParams(dimension_semantics=("arbitrary", "arbitrary"),
                                             vmem_limit_bytes=VMEM_LIMIT),
        name="hgrn",
    )(hgp, hgp, hgp, hgp, lb_logits, out_gain, mstack, lvl)


def _dsa_kernel(topk, qn_ref, iqs_ref, sg_ref, mqn_ref, ikT_ref, knT_ref, v_ref, kmT_ref,
                vm_ref, e2_ref, sut_ref, o_ref,
                key_scr, sgn_scr, lhs_scr, q4_scr, m_scr, l_scr, acc_scr):
    R, KC = Q_ROWS, KEY_CHUNK
    blk = pl.program_id(0)
    nch = blk // (KC // R) + 1
    row = blk * R + lax.broadcasted_iota(i32, (R, KC), 0)
    col0 = lax.broadcasted_iota(i32, (R, KC), 1)

    sgn_scr[...] = _dot(sg_ref[...], e2_ref[...])
    for h in range(IDX_HEADS):
        lhs_scr[h * R:(h + 1) * R, :] = iqs_ref[:, h * IDX_DH:(h + 1) * IDX_DH]
    for h in range(SA_HEADS):
        q4_scr[h] = qn_ref[:, h * SA_DH:(h + 1) * SA_DH]

    def score_chunk(c, carry):
        k0 = pl.multiple_of(c * KC, KC)
        ik = ikT_ref[:, pl.ds(k0, KC)]
        acc = jnp.zeros((R, KC), f32)
        for h in range(IDX_HEADS):
            s = _dot(lhs_scr[h * R:(h + 1) * R, :], ik)
            sgn = sgn_scr[:, h * LANES:(h + 1) * LANES]
            acc = acc + jnp.maximum(s, 0.0) * jnp.concatenate([sgn] * (KC // LANES), axis=1)
        bits = pltpu.bitcast(acc, i32)
        keys = bits ^ ((bits >> 31) & 0x7FFFFFFF)
        key_scr[:, pl.ds(k0, KC)] = jnp.where(col0 + k0 <= row, keys, INT_MIN)
        return carry

    lax.fori_loop(0, nch, score_chunk, 0)

    def count_ge(cand):
        candb = jnp.broadcast_to(cand, (R, LANES))

        def body(c, cnt):
            k0 = pl.multiple_of(c * KC, KC)
            kk = key_scr[:, pl.ds(k0, KC)]
            for g in range(KC // LANES):
                cnt = cnt + jnp.where(kk[:, g * LANES:(g + 1) * LANES] >= candb, 1.0, 0.0)
            return cnt

        cnt = lax.fori_loop(0, nch, body, jnp.zeros((R, LANES), f32))
        return jnp.sum(cnt, axis=1, keepdims=True)

    def bisect(b, thr):
        cand = thr + jnp.left_shift(jnp.int32(1), 31 - b)
        return jnp.where(count_ge(cand) >= topk, cand, thr)

    thr = lax.fori_loop(0, 32, bisect, jnp.full((R, 1), INT_MIN, i32))
    thr = jnp.maximum(thr, INT_MIN + 1)
    need = topk - count_ge(thr + 1)

    m_scr[...] = jnp.full_like(m_scr, NEG_BIG)
    l_scr[...] = jnp.zeros_like(l_scr)
    acc_scr[...] = jnp.zeros_like(acc_scr)

    def attend_chunk(c, ties):
        k0 = pl.multiple_of(c * KC, KC)
        kk = key_scr[:, pl.ds(k0, KC)]
        eq = jnp.where(kk == thr, 1.0, 0.0)
        before = _dot(eq.astype(bf16), sut_ref[...]) + ties
        sel = kk >= jnp.where(before < need, thr, thr + 1)
        vch = v_ref[pl.ds(k0, KC), :]
        for h in range(SA_HEADS):
            lg = _dot(q4_scr[h], knT_ref[h * SA_DH:(h + 1) * SA_DH, pl.ds(k0, KC)])
            lg = jnp.where(sel, lg, NEG_BIG)
            m_old = m_scr[h]
            m_new = jnp.maximum(m_old, jnp.max(lg, axis=1, keepdims=True))
            alpha = jnp.exp(m_old - m_new)
            p = jnp.exp(lg - m_new)
            l_scr[h] = alpha * l_scr[h] + jnp.sum(p, axis=1, keepdims=True)
            acc_scr[h] = alpha * acc_scr[h] + _dot(p.astype(bf16), vch)
            m_scr[h] = m_new
        return ties + jnp.sum(eq, axis=1, keepdims=True)

    lax.fori_loop(0, nch, attend_chunk, jnp.zeros((R, 1), f32))

    lane_head = lax.broadcasted_iota(i32, (R, SA_WIDTH), 1) // SA_DH
    o_sa = jnp.zeros((R, SA_WIDTH), f32)
    for h in range(SA_HEADS):
        o_sa = jnp.where(lane_head == h, acc_scr[h] / l_scr[h], o_sa)

    vm = vm_ref[...]
    o_mem = jnp.zeros((R, MEM_WIDTH), f32)
    for h in range(MEM_HEADS):
        lg = _dot(mqn_ref[:, h * MEM_DH:(h + 1) * MEM_DH], kmT_ref[h * MEM_DH:(h + 1) * MEM_DH, :])
        p = jnp.exp(lg - jnp.max(lg, axis=1, keepdims=True))
        oh = _dot(p.astype(bf16), vm) / jnp.sum(p, axis=1, keepdims=True)
        o_mem = jnp.where(lane_head == h, oh, o_mem)

    o_ref[...] = jnp.concatenate([o_sa, o_mem], axis=1).astype(bf16)


def _dsa(qn, iqs, sg, mqn, ikT, knT, v, kmT, vm, e2, sut):
    L = qn.shape[0]
    n_mem = vm.shape[0]
    topk = min(TOPK_MAX, L // 4)
    R = Q_ROWS
    Lp = -(-L // KEY_CHUNK) * KEY_CHUNK
    rows = lambda w: pl.BlockSpec((R, w), lambda i: (i, 0))
    return pl.pallas_call(
        functools.partial(_dsa_kernel, topk),
        grid=(L // R,),
        in_specs=[rows(SA_WIDTH), rows(IDX_WIDTH), rows(LANES), rows(MEM_WIDTH),
                  _resident((IDX_DH, Lp)), _resident((SA_WIDTH, Lp)), _resident((Lp, SA_WIDTH)),
                  _resident((MEM_WIDTH, n_mem)), _resident((n_mem, MEM_WIDTH)),
                  _resident(e2.shape), _resident(sut.shape)],
        out_specs=rows(SA_WIDTH + MEM_WIDTH),
        out_shape=jax.ShapeDtypeStruct((L, SA_WIDTH + MEM_WIDTH), bf16),
        scratch_shapes=[pltpu.VMEM((R, Lp), i32),
                        pltpu.VMEM((R, IDX_HEADS * LANES), f32),
                        pltpu.VMEM((IDX_HEADS * R, IDX_DH), bf16),
                        pltpu.VMEM((SA_HEADS, R, SA_DH), bf16),
                        pltpu.VMEM((SA_HEADS, R, 1), f32),
                        pltpu.VMEM((SA_HEADS, R, 1), f32),
                        pltpu.VMEM((SA_HEADS, R, SA_WIDTH), f32)],
        compiler_params=pltpu.CompilerParams(dimension_semantics=("arbitrary",),
                                             vmem_limit_bytes=VMEM_LIMIT),
        name="dsa",
    )(qn, iqs, sg, mqn, ikT, knT, v, kmT, vm, e2, sut)


def _outffn_kernel(x_ref, ohg_ref, osm_ref, wo_ref, g_ref, wi_ref, wf_ref, o_ref):
    x1 = (x_ref[...] + _dot(ohg_ref[...], wo_ref[:HG_WIDTH, :])
          + _dot(osm_ref[...], wo_ref[HG_WIDTH:, :]))
    h = _rmsnorm_rows(x1, g_ref[...]).astype(bf16)
    a = _dot(h, wi_ref[:, :D_FF])
    b = _dot(h, wi_ref[:, D_FF:])
    y = (a * jax.nn.sigmoid(a) * b).astype(bf16)
    o_ref[...] = x1 + _dot(y, wf_ref[...])


def _outffn(x2d, o_hg, o_sm, w_out, g_ffn, w_ffn_in, w_ffn_out):
    L = x2d.shape[0]
    tm = FFN_ROWS
    rows = lambda w: pl.BlockSpec((tm, w), lambda i: (i, 0))
    return pl.pallas_call(
        _outffn_kernel,
        grid=(L // tm,),
        in_specs=[rows(D_MODEL), rows(HG_WIDTH), rows(SA_WIDTH + MEM_WIDTH),
                  _resident((D_MODEL, D_MODEL)), _resident((1, D_MODEL)),
                  _resident((D_MODEL, 2 * D_FF)), _resident((D_FF, D_MODEL))],
        out_specs=rows(D_MODEL),
        out_shape=jax.ShapeDtypeStruct((L, D_MODEL), f32),
        compiler_params=pltpu.CompilerParams(dimension_semantics=("arbitrary",),
                                             vmem_limit_bytes=VMEM_LIMIT),
        name="outffn",
    )(x2d, o_hg, o_sm, w_out, g_ffn, w_ffn_in, w_ffn_out)


def _pad_keys(a, axis, Lp):
    pad = Lp - a.shape[axis]
    if pad == 0:
        return a
    widths = [(0, 0)] * a.ndim
    widths[axis] = (0, pad)
    return jnp.pad(a, widths)


def kernel(x, mem, w_in, w_out, w_mem_kv, lb_logits, norm_mix, norm_mem, norm_ffn, hg_out_gain,
           sa_q_gain, sa_k_gain, mem_q_gain, mem_k_gain, w_ffn_in, w_ffn_out):
    B, L, _ = x.shape
    assert B == 1 and L % IN_ROWS == 0
    depth = w_in.shape[0]
    Lp = -(-L // KEY_CHUNK) * KEY_CHUNK

    grp = jnp.asarray(_group_matrix(SA_WIDTH, SA_DH), bf16)
    expm = jnp.asarray(_iw_expand(IDX_DH), bf16)
    e2 = jnp.asarray(_iw_expand(LANES), bf16)
    sut = jnp.asarray(_strict_upper(KEY_CHUNK), bf16)
    ms_np, lvl_np = _hgrn_constants()
    mstack = jnp.asarray(ms_np, bf16)
    lvl = jnp.asarray(lvl_np, i32)

    x2d = x[0]
    mem2d = mem[0]
    ik0 = IN_MAIN - MEM_WIDTH
    for layer in range(depth):
        w = w_in[layer]
        w_perm = jnp.concatenate(
            [w[:, :ik0], w[:, ik0 + IDX_DH + IDX_HEADS:], w[:, ik0:ik0 + IDX_DH + IDX_HEADS],
             jnp.zeros((D_MODEL, IN_PAD - w.shape[1]), w.dtype)], axis=1).astype(bf16)
        tile4 = lambda g: jnp.tile(g[layer], 4)[None, :]
        kmT, vm = _memkv(mem2d, norm_mem[layer][None, :], w_mem_kv[layer].astype(bf16),
                         tile4(mem_k_gain), grp)
        hgp, qn, knT, v, iqs, ikT, sg, mqn = _inproj(
            x2d, norm_mix[layer][None, :], w_perm, tile4(sa_q_gain), tile4(sa_k_gain),
            tile4(mem_q_gain), grp, expm)
        o_hg = _hgrn(layer, hgp, lb_logits, hg_out_gain[layer][None, :], mstack, lvl)
        o_sm = _dsa(qn, iqs, sg, mqn, _pad_keys(ikT, 1, Lp), _pad_keys(knT, 1, Lp),
                    _pad_keys(v, 0, Lp), kmT, vm, e2, sut)
        x2d = _outffn(x2d, o_hg, o_sm, w_out[layer].astype(bf16), norm_ffn[layer][None, :],
                      w_ffn_in[layer].astype(bf16), w_ffn_out[layer].astype(bf16))
    return x2d[None]
```

```python
import functools
import math

import numpy as np
import jax
import jax.numpy as jnp
from jax import lax
from jax.experimental import pallas as pl
from jax.experimental.pallas import tpu as pltpu

f32 = jnp.float32
bf16 = jnp.bfloat16
i32 = jnp.int32

D_MODEL = 1024
HG_HEADS = 4
HG_DK = 128
HG_DV = 128
HG_WIDTH = HG_HEADS * HG_DV
SA_HEADS = 4
SA_DH = 64
SA_WIDTH = SA_HEADS * SA_DH
MEM_HEADS = 4
MEM_DH = 64
MEM_WIDTH = MEM_HEADS * MEM_DH
IDX_HEADS = 8
IDX_DH = 64
IDX_WIDTH = IDX_HEADS * IDX_DH
TOPK_MAX = 256
D_FF = 2816
EPS = 1e-6

LANES = 128
SUBLANES = 8
WORD_BITS = 32
HG_IN = 4 * HG_WIDTH
IN_MAIN = HG_IN + 3 * SA_WIDTH + IDX_WIDTH + MEM_WIDTH
IN_PAD = IN_MAIN + LANES
IW_LANE0 = IDX_DH

HG_CHUNK = 128
HG_LEVELS = 7
Q_ROWS = LANES
KEY_TILE = 128
KEY_GROUP = WORD_BITS * SUBLANES
GROUP_UNROLL = 8
KEY_CHUNK = 2 * KEY_GROUP
SUPER_KEYS = 4 * KEY_CHUNK
IN_ROWS = 512
FFN_ROWS = 256

INT_MIN = -2 ** 31
NEG_BIG = -1e30
LOG2E = math.log2(math.e)
VMEM_LIMIT = 56 * 1024 * 1024


def _dot(a, b):
    return jnp.dot(a, b, preferred_element_type=f32)


def _dot_nt(a, b):
    return lax.dot_general(a, b, (((1,), (1,)), ((), ())), preferred_element_type=f32)


def _dot_tn(a, b):
    return lax.dot_general(a, b, (((0,), (0,)), ((), ())), preferred_element_type=f32)


def _split3(x):
    a = x.astype(bf16)
    r = x - a.astype(f32)
    b = r.astype(bf16)
    c = (r - b.astype(f32)).astype(bf16)
    return a, b, c


def _dot_sel_right(x, m01):
    a, b, c = _split3(x)
    return _dot(a, m01) + _dot(b, m01) + _dot(c, m01)


def _dot_sel_left(m01, x):
    a, b, c = _split3(x)
    return _dot(m01, a) + _dot(m01, b) + _dot(m01, c)


def _rmsnorm_rows(x, gain):
    ms = jnp.mean(x * x, axis=-1, keepdims=True)
    return x * lax.rsqrt(ms + EPS) * gain


def _headnorm(t, gain, grp, width):
    ss = _dot_sel_right(t * t, grp)
    return t * lax.rsqrt(ss * (1.0 / width) + EPS) * gain


def _resident(shape):
    zeros = (0,) * len(shape)
    return pl.BlockSpec(shape, lambda *_: zeros, pipeline_mode=pl.Buffered(1))


def _group_matrix(n, width):
    g = np.arange(n) // width
    return (g[:, None] == g[None, :]).astype(np.float32)


def _iw_expand(width):
    m = np.zeros((LANES, IDX_HEADS * width), np.float32)
    for h in range(IDX_HEADS):
        m[IW_LANE0 + h, h * width:(h + 1) * width] = 1.0
    return m


def _strict_lower(n):
    a = np.arange(n)
    return (a[:, None] > a[None, :]).astype(np.float32)


def _hgrn_constants():
    n = HG_CHUNK
    t = np.arange(n)[:, None]
    r = np.arange(n)[None, :]
    mats = []
    for lv in range(2, HG_LEVELS + 2):
        h = 1 << (lv - 1)
        mats.append(((r >= (t // h) * h) & (r <= t)).astype(np.float32))
        mats.append(((r > t) & (r <= (t // h) * h + h - 1)).astype(np.float32))
    stack = np.concatenate(mats, axis=0)
    x = t ^ r
    lvl = np.where(x == 0, 0, np.floor(np.log2(np.maximum(x, 1))).astype(np.int64) + 1)
    lvl = np.where(r > t, -1, lvl).astype(np.int32)
    return stack, lvl


def _memkv_kernel(mem_ref, g_ref, w_ref, gk_ref, grp_ref, km_ref, vmT_ref):
    mn = _rmsnorm_rows(mem_ref[...], g_ref[...]).astype(bf16)
    kv = _dot(mn, w_ref[...])
    km_ref[...] = _headnorm(kv[:, :MEM_WIDTH], gk_ref[...], grp_ref[...], MEM_DH).astype(bf16)
    vmT_ref[...] = kv[:, MEM_WIDTH:].T.astype(bf16)


def _memkv(mem2d, g_mem, w_kv, gk, grp):
    m = mem2d.shape[0]
    return pl.pallas_call(
        _memkv_kernel,
        out_shape=(jax.ShapeDtypeStruct((m, MEM_WIDTH), bf16),
                   jax.ShapeDtypeStruct((MEM_WIDTH, m), bf16)),
        name="memkv",
    )(mem2d, g_mem, w_kv, gk, grp)


def _inproj_kernel(x_ref, g_ref, w_ref, gq_ref, gk_ref, gm_ref, grp_ref, exp_ref,
                   hgp_ref, qT_ref, k_ref, vT_ref, iqT_ref, ik_ref, sgT_ref, mqT_ref):
    h = _rmsnorm_rows(x_ref[...], g_ref[...]).astype(bf16)
    p = _dot(h, w_ref[...])
    hgp_ref[...] = p[:, :HG_IN]
    o = HG_IN
    sq = p[:, o:o + SA_WIDTH]
    sk = p[:, o + SA_WIDTH:o + 2 * SA_WIDTH]
    sv = p[:, o + 2 * SA_WIDTH:o + 3 * SA_WIDTH]
    o += 3 * SA_WIDTH
    iq = p[:, o:o + IDX_WIDTH]
    o += IDX_WIDTH
    mq = p[:, o:o + MEM_WIDTH]
    last = p[:, IN_MAIN:IN_PAD]
    grp = grp_ref[...]
    qn = _headnorm(sq, gq_ref[...], grp, SA_DH) * (SA_DH ** -0.5 * LOG2E)
    qT_ref[...] = qn.T.astype(bf16)
    k_ref[...] = _headnorm(sk, gk_ref[...], grp, SA_DH).astype(bf16)
    vT_ref[...] = sv.T.astype(bf16)
    wabs = _dot_sel_right(jnp.abs(last), exp_ref[...]) * (IDX_HEADS ** -0.5 * IDX_DH ** -0.5)
    iqT_ref[...] = (iq * wabs).T.astype(bf16)
    ik_ref[...] = last[:, :IDX_DH].astype(bf16)
    sgn = jnp.where(last >= 0.0, 1.0, -1.0).T
    sgT_ref[...] = sgn[IW_LANE0:IW_LANE0 + IDX_HEADS, :]
    mqn = _headnorm(mq, gm_ref[...], grp, MEM_DH) * (MEM_DH ** -0.5 * LOG2E)
    mqT_ref[...] = mqn.T.astype(bf16)


def _inproj(x2d, g_mix, w_in, gq, gk, gm, grp, expm):
    L = x2d.shape[0]
    tm = IN_ROWS
    rows = lambda w: pl.BlockSpec((tm, w), lambda i: (i, 0))
    cols = lambda h: pl.BlockSpec((h, tm), lambda i: (0, i))
    return pl.pallas_call(
        _inproj_kernel,
        grid=(L // tm,),
        in_specs=[rows(D_MODEL), _resident((1, D_MODEL)), _resident((D_MODEL, IN_PAD)),
                  _resident((1, SA_WIDTH)), _resident((1, SA_WIDTH)), _resident((1, MEM_WIDTH)),
                  _resident((SA_WIDTH, SA_WIDTH)), _resident((LANES, IDX_WIDTH))],
        out_specs=(rows(HG_IN), cols(SA_WIDTH), rows(SA_WIDTH), cols(SA_WIDTH),
                   cols(IDX_WIDTH), rows(IDX_DH), cols(IDX_HEADS), cols(MEM_WIDTH)),
        out_shape=(jax.ShapeDtypeStruct((L, HG_IN), f32),
                   jax.ShapeDtypeStruct((SA_WIDTH, L), bf16),
                   jax.ShapeDtypeStruct((L, SA_WIDTH), bf16),
                   jax.ShapeDtypeStruct((SA_WIDTH, L), bf16),
                   jax.ShapeDtypeStruct((IDX_WIDTH, L), bf16),
                   jax.ShapeDtypeStruct((L, IDX_DH), bf16),
                   jax.ShapeDtypeStruct((IDX_HEADS, L), f32),
                   jax.ShapeDtypeStruct((MEM_WIDTH, L), bf16)),
        compiler_params=pltpu.CompilerParams(dimension_semantics=("arbitrary",),
                                             vmem_limit_bytes=VMEM_LIMIT),
        name="inproj",
    )(x2d, g_mix, w_in, gq, gk, gm, grp, expm)


def _hgrn_kernel(layer, q_ref, f_ref, i_ref, g_ref, lbl_ref, og_ref, ms_ref, lvl_ref,
                 o_ref, st_ref):
    @pl.when(pl.program_id(1) == 0)
    def _():
        st_ref[...] = jnp.zeros_like(st_ref)

    lg = lbl_ref[...]
    e = jnp.exp(lg - jnp.max(lg, axis=0, keepdims=True))
    sm = e / jnp.sum(e, axis=0, keepdims=True)
    lb = jnp.zeros((1, HG_DK), f32)
    for j in range(1, layer + 1):
        lb = lb + sm[j:j + 1, :]

    z = f_ref[...]
    log_sig = jnp.minimum(z, 0.0) - jnp.log1p(jnp.exp(-jnp.abs(z)))
    c = jnp.log1p(-lb) + log_sig
    a = jnp.log(lb)
    lf = jnp.maximum(a, c) + jnp.log1p(jnp.exp(-jnp.abs(a - c)))
    kk = (1.0 - lb) * jax.nn.sigmoid(-z)
    q = q_ref[...]
    qf = q * jax.nn.sigmoid(q)
    v = i_ref[...].astype(bf16)

    n = HG_CHUNK
    ex = _dot_sel_left(ms_ref[...], lf)
    lvl = lvl_ref[...]
    kb = kk.astype(bf16)
    att = jnp.where(lvl == 0, _dot_nt(qf.astype(bf16), kb), 0.0)
    att = jnp.where(lvl == 1, _dot_nt((qf * jnp.exp(lf)).astype(bf16), kb), att)
    for lv in range(2, HG_LEVELS + 1):
        eq = ex[(2 * lv - 4) * n:(2 * lv - 3) * n]
        ek = ex[(2 * lv - 3) * n:(2 * lv - 2) * n]
        part = _dot_nt((qf * jnp.exp(eq)).astype(bf16), (kk * jnp.exp(ek)).astype(bf16))
        att = jnp.where(lvl == lv, part, att)
    top = 2 * HG_LEVELS - 2
    eb = ex[top * n:(top + 1) * n]
    ekd = ex[(top + 1) * n:(top + 2) * n]
    st = st_ref[...]
    o = _dot_nt((qf * jnp.exp(eb)).astype(bf16), st.astype(bf16)) + _dot(att.astype(bf16), v)
    st_ref[...] = st * jnp.exp(eb[n - 1:n, :]) + _dot_tn(v, (kk * jnp.exp(ekd)).astype(bf16))

    gate = g_ref[...]
    o_ref[...] = (_rmsnorm_rows(o, og_ref[...]) * (gate * jax.nn.sigmoid(gate))).astype(bf16)


def _hgrn(layer, hgp, lb_logits, out_gain, mstack, lvl):
    L = hgp.shape[0]
    depth = lb_logits.shape[0]
    n = HG_CHUNK
    col = lambda g: pl.BlockSpec((n, HG_DK), lambda h, t, g=g: (t, g * HG_HEADS + h))
    return pl.pallas_call(
        functools.partial(_hgrn_kernel, layer),
        grid=(HG_HEADS, L // n),
        in_specs=[col(0), col(1), col(2), col(3),
                  pl.BlockSpec((depth, HG_DK), lambda h, t: (0, h)),
                  _resident((1, HG_DV)), _resident(mstack.shape), _resident((n, n))],
        out_specs=pl.BlockSpec((n, HG_DV), lambda h, t: (t, h)),
        out_shape=jax.ShapeDtypeStruct((L, HG_WIDTH), bf16),
        scratch_shapes=[pltpu.VMEM((HG_DV, HG_DK), f32)],
        compiler_params=pltpu.CompilerParams(dimension_semantics=("arbitrary", "arbitrary"),
                                             vmem_limit_bytes=VMEM_LIMIT),
        name="hgrn",
    )(hgp, hgp, hgp, hgp, lb_logits, out_gain, mstack, lvl)


def _bit_transpose32(w):
    w = list(w)
    j, m = 16, 0x0000FFFF
    while j:
        k = 0
        while k < WORD_BITS:
            t = (w[k] ^ lax.shift_right_logical(w[k + j], jnp.full_like(w[k], j))) & m
            w[k] = w[k] ^ t
            w[k + j] = w[k + j] ^ lax.shift_left(t, jnp.full_like(t, j))
            k = (k + j + 1) & ~j
        j >>= 1
        m = (m ^ (m << j)) & 0xFFFFFFFF
    return w


def _pair_blockdiag(ref, pair, width):
    z = jnp.zeros((width, LANES), bf16)
    a = ref[(2 * pair) * width:(2 * pair + 1) * width, :]
    b = ref[(2 * pair + 1) * width:(2 * pair + 2) * width, :]
    return jnp.concatenate([jnp.concatenate([a, z], axis=1), jnp.concatenate([z, b], axis=1)], axis=0)


def _dsa_kernel(topk, qT_ref, iqT_ref, sgT_ref, mqT_ref, ik_ref, k_ref, vT_ref, km_ref, vmT_ref,
                slt_ref, o_ref,
                key_scr, plane_scr, alive_scr, rhs_scr, qbd_scr, lg_scr, p_scr, acc_scr):
    R, G, U, CH = Q_ROWS, KEY_GROUP, GROUP_UNROLL, KEY_CHUNK
    SCK = lg_scr.shape[1]
    blk = pl.program_id(0)
    ntile = blk + 1
    nchunk = (blk * R + R + CH - 1) // CH
    ngrp = nchunk * (CH // G)
    niter = (ngrp + U - 1) // U
    row = blk * R + lax.broadcasted_iota(i32, (KEY_TILE, R), 1)
    sub = lax.broadcasted_iota(i32, (KEY_TILE, R), 0)

    for p in range(IDX_HEADS // 2):
        rhs_scr[p] = jnp.concatenate([iqT_ref[(2 * p) * IDX_DH:(2 * p + 1) * IDX_DH, :],
                                      iqT_ref[(2 * p + 1) * IDX_DH:(2 * p + 2) * IDX_DH, :]], axis=1)
    for p in range(SA_HEADS // 2):
        qbd_scr[p] = _pair_blockdiag(qT_ref, p, SA_DH)
    sgn = sgT_ref[...]

    def score_group(g):
        words = []
        for t in range(G // KEY_TILE):
            k0 = pl.multiple_of(g * G + t * KEY_TILE, KEY_TILE)
            ikt = ik_ref[pl.ds(k0, KEY_TILE), :]
            acc = jnp.zeros((KEY_TILE, R), f32)
            for p in range(IDX_HEADS // 2):
                s2 = _dot(ikt, rhs_scr[p])
                acc = acc + jnp.maximum(s2[:, :R], 0.0) * sgn[2 * p:2 * p + 1, :]
                acc = acc + jnp.maximum(s2[:, R:], 0.0) * sgn[2 * p + 1:2 * p + 2, :]
            bits = pltpu.bitcast(acc, i32)
            keys = bits ^ ((bits >> 31) & 0x7FFFFFFF)
            keys = jnp.where(sub + k0 <= row, keys, INT_MIN)
            key_scr[pl.ds(k0, KEY_TILE), :] = keys
            u = keys ^ INT_MIN
            words += [u[j * SUBLANES:(j + 1) * SUBLANES, :] for j in range(KEY_TILE // SUBLANES)]
        planes = _bit_transpose32(words)
        for b in range(WORD_BITS):
            plane_scr[g, b] = planes[b]
        alive_scr[g] = jnp.full((SUBLANES, R), -1, i32)

    def score_chunk(c, carry):
        for gg in range(CH // G):
            score_group(c * (CH // G) + gg)
        return carry

    lax.fori_loop(0, nchunk, score_chunk, 0)

    def clear_group(g, carry):
        alive_scr[g] = jnp.zeros((SUBLANES, R), i32)
        for b in range(WORD_BITS):
            plane_scr[g, b] = jnp.zeros((SUBLANES, R), i32)
        return carry

    lax.fori_loop(ngrp, niter * U, clear_group, 0)

    def count_pass(p, tm_prev):
        tmb = None if tm_prev is None else jnp.broadcast_to(tm_prev, (SUBLANES, R))

        def body(it, pc):
            for u in range(U):
                g = it * U + u
                a = alive_scr[g]
                if tmb is not None:
                    a = a & ~(plane_scr[g, p - 1] ^ tmb)
                    alive_scr[g] = a
                pc = pc + lax.population_count(a & plane_scr[g, p])
            return pc

        pc = lax.fori_loop(0, niter, body, jnp.zeros((SUBLANES, R), i32))
        return jnp.sum(pc.astype(f32), axis=0, keepdims=True)

    def decide(p, cnt, above, thr):
        take = above + cnt >= topk
        tm = jnp.where(take, -1, 0).astype(i32)
        thr = thr | (tm & jnp.left_shift(jnp.int32(1), WORD_BITS - 1 - p))
        above = jnp.where(take, above, above + cnt)
        return tm, above, thr

    def radix_pass(p, carry):
        tm, above, thr = carry
        return decide(p, count_pass(p, tm), above, thr)

    zero = jnp.zeros((1, R), f32)
    carry = decide(0, count_pass(0, None), zero, jnp.zeros((1, R), i32))
    tm, above, thr_u = lax.fori_loop(1, WORD_BITS, radix_pass, carry)

    tmb = jnp.broadcast_to(tm, (SUBLANES, R))

    def tie_body(it, pc):
        for u in range(U):
            g = it * U + u
            pc = pc + lax.population_count(alive_scr[g] & ~(plane_scr[g, WORD_BITS - 1] ^ tmb))
        return pc

    ties = jnp.sum(lax.fori_loop(0, niter, tie_body, jnp.zeros((SUBLANES, R), i32)).astype(f32),
                   axis=0, keepdims=True)
    need = topk - above
    thr = jnp.maximum(thr_u ^ INT_MIN, INT_MIN + 1)

    excess = jnp.max(jnp.where((ties > need) & (thr_u != 0), 1.0, 0.0)) > 0.0

    @pl.when(excess)
    def _():
        def fix_tile(t, seen):
            k0 = pl.multiple_of(t * KEY_TILE, KEY_TILE)
            keys = key_scr[pl.ds(k0, KEY_TILE), :]
            eq = jnp.where(keys == thr, 1.0, 0.0)
            before = _dot(slt_ref[...], eq.astype(bf16)) + seen
            drop = eq * jnp.where(before >= need, 1.0, 0.0)
            key_scr[pl.ds(k0, KEY_TILE), :] = jnp.where(drop > 0.0, INT_MIN, keys)
            return seen + jnp.sum(eq, axis=0, keepdims=True)

        lax.fori_loop(0, ntile, fix_tile, zero)

    acc_scr[...] = jnp.zeros_like(acc_scr)
    nper = SCK // CH

    def fold(x, op):
        return op(x.reshape(x.shape[0] // SUBLANES, SUBLANES, R), axis=0)

    def attend_super(sc, carry):
        m_old, l_old = carry
        c_cnt = jnp.minimum(nper, nchunk - sc * nper)
        key0 = sc * SCK

        def logits_chunk(ci, mx):
            mx = list(mx)
            for gg in range(CH // G):
                off = pl.multiple_of(ci * CH + gg * G, G)
                k0 = pl.multiple_of(key0 + off, G)
                sel = key_scr[pl.ds(k0, G), :] >= thr
                kt = k_ref[pl.ds(k0, G), :]
                for p in range(SA_HEADS // 2):
                    lg2 = _dot(kt[:, p * 2 * SA_DH:(p + 1) * 2 * SA_DH], qbd_scr[p])
                    for hh in range(2):
                        h = 2 * p + hh
                        lg = jnp.where(sel, lg2[:, hh * R:(hh + 1) * R], NEG_BIG)
                        lg_scr[h, pl.ds(off, G), :] = lg
                        mx[h] = jnp.maximum(mx[h], fold(lg, jnp.max))
            return tuple(mx)

        neg = jnp.full((SUBLANES, R), NEG_BIG, f32)
        mx = lax.fori_loop(0, c_cnt, logits_chunk, (neg,) * SA_HEADS)
        m_new = jnp.maximum(m_old, jnp.concatenate(
            [jnp.max(t, axis=0, keepdims=True) for t in mx], axis=0))
        alpha = jnp.exp2(m_old - m_new)

        def exp_chunk(ci, ls):
            ls = list(ls)
            off = pl.multiple_of(ci * CH, CH)
            for h in range(SA_HEADS):
                pr = jnp.exp2(lg_scr[h, pl.ds(off, CH), :] - m_new[h:h + 1, :])
                ls[h] = ls[h] + fold(pr, jnp.sum)
                p_scr[h, pl.ds(off, CH), :] = pr.astype(bf16)
            return tuple(ls)

        ls = lax.fori_loop(0, c_cnt, exp_chunk, (jnp.zeros((SUBLANES, R), f32),) * SA_HEADS)

        def zero_chunk(ci, c):
            off = pl.multiple_of(ci * CH, CH)
            for h in range(SA_HEADS):
                p_scr[h, pl.ds(off, CH), :] = jnp.zeros((CH, R), bf16)
            return c

        lax.fori_loop(c_cnt, nper, zero_chunk, 0)

        for h in range(SA_HEADS):
            pv = _dot(vT_ref[h * SA_DH:(h + 1) * SA_DH, pl.ds(pl.multiple_of(key0, SCK), SCK)],
                      p_scr[h])
            acc_scr[h] = alpha[h:h + 1, :] * acc_scr[h] + pv
        l_new = alpha * l_old + jnp.concatenate(
            [jnp.sum(t, axis=0, keepdims=True) for t in ls], axis=0)
        return m_new, l_new

    nsuper = (nchunk + nper - 1) // nper
    _, l_fin = lax.fori_loop(0, nsuper, attend_super,
                             (jnp.full((SA_HEADS, R), NEG_BIG, f32), jnp.zeros((SA_HEADS, R), f32)))

    outs = [acc_scr[h] / l_fin[h:h + 1, :] for h in range(SA_HEADS)]

    km = km_ref[...]
    for p in range(MEM_HEADS // 2):
        lg2 = _dot(km[:, p * 2 * MEM_DH:(p + 1) * 2 * MEM_DH], _pair_blockdiag(mqT_ref, p, MEM_DH))
        for hh in range(2):
            h = 2 * p + hh
            lg = lg2[:, hh * R:(hh + 1) * R]
            pr = jnp.exp2(lg - jnp.max(lg, axis=0, keepdims=True))
            pv = _dot(vmT_ref[h * MEM_DH:(h + 1) * MEM_DH, :], pr.astype(bf16))
            outs.append(pv / jnp.sum(pr, axis=0, keepdims=True))

    o_ref[...] = jnp.concatenate(outs, axis=0).T.astype(bf16)


def _dsa(qT, iqT, sgT, mqT, ik, k, vT, km, vmT, slt):
    L = k.shape[0]
    n_mem = km.shape[0]
    topk = min(TOPK_MAX, L // 4)
    R = Q_ROWS
    ngroups = L // KEY_GROUP + GROUP_UNROLL
    sck = min(SUPER_KEYS, L)
    assert L % sck == 0 and sck % KEY_CHUNK == 0
    cols = lambda h: pl.BlockSpec((h, R), lambda i: (0, i))
    return pl.pallas_call(
        functools.partial(_dsa_kernel, topk),
        grid=(L // R,),
        in_specs=[cols(SA_WIDTH), cols(IDX_WIDTH), cols(IDX_HEADS), cols(MEM_WIDTH),
                  _resident((L, IDX_DH)), _resident((L, SA_WIDTH)), _resident((SA_WIDTH, L)),
                  _resident((n_mem, MEM_WIDTH)), _resident((MEM_WIDTH, n_mem)),
                  _resident((KEY_TILE, KEY_TILE))],
        out_specs=pl.BlockSpec((R, SA_WIDTH + MEM_WIDTH), lambda i: (i, 0)),
        out_shape=jax.ShapeDtypeStruct((L, SA_WIDTH + MEM_WIDTH), bf16),
        scratch_shapes=[pltpu.VMEM((L, R), i32),
                        pltpu.VMEM((ngroups, WORD_BITS, SUBLANES, R), i32),
                        pltpu.VMEM((ngroups, SUBLANES, R), i32),
                        pltpu.VMEM((IDX_HEADS // 2, IDX_DH, 2 * R), bf16),
                        pltpu.VMEM((SA_HEADS // 2, 2 * SA_DH, 2 * R), bf16),
                        pltpu.VMEM((SA_HEADS, sck, R), f32),
                        pltpu.VMEM((SA_HEADS, sck, R), bf16),
                        pltpu.VMEM((SA_HEADS, SA_DH, R), f32)],
        compiler_params=pltpu.CompilerParams(dimension_semantics=("arbitrary",),
                                             vmem_limit_bytes=VMEM_LIMIT),
        name="dsa",
    )(qT, iqT, sgT, mqT, ik, k, vT, km, vmT, slt)


def _outffn_kernel(x_ref, ohg_ref, osm_ref, wo_ref, g_ref, wi_ref, wf_ref, o_ref):
    x1 = (x_ref[...] + _dot(ohg_ref[...], wo_ref[:HG_WIDTH, :])
          + _dot(osm_ref[...], wo_ref[HG_WIDTH:, :]))
    h = _rmsnorm_rows(x1, g_ref[...]).astype(bf16)
    a = _dot(h, wi_ref[:, :D_FF])
    b = _dot(h, wi_ref[:, D_FF:])
    y = (a * jax.nn.sigmoid(a) * b).astype(bf16)
    o_ref[...] = x1 + _dot(y, wf_ref[...])


def _outffn(x2d, o_hg, o_sm, w_out, g_ffn, w_ffn_in, w_ffn_out):
    L = x2d.shape[0]
    tm = FFN_ROWS
    rows = lambda w: pl.BlockSpec((tm, w), lambda i: (i, 0))
    return pl.pallas_call(
        _outffn_kernel,
        grid=(L // tm,),
        in_specs=[rows(D_MODEL), rows(HG_WIDTH), rows(SA_WIDTH + MEM_WIDTH),
                  _resident((D_MODEL, D_MODEL)), _resident((1, D_MODEL)),
                  _resident((D_MODEL, 2 * D_FF)), _resident((D_FF, D_MODEL))],
        out_specs=rows(D_MODEL),
        out_shape=jax.ShapeDtypeStruct((L, D_MODEL), f32),
        compiler_params=pltpu.CompilerParams(dimension_semantics=("arbitrary",),
                                             vmem_limit_bytes=VMEM_LIMIT),
        name="outffn",
    )(x2d, o_hg, o_sm, w_out, g_ffn, w_ffn_in, w_ffn_out)


def kernel(x, mem, w_in, w_out, w_mem_kv, lb_logits, norm_mix, norm_mem, norm_ffn, hg_out_gain,
           sa_q_gain, sa_k_gain, mem_q_gain, mem_k_gain, w_ffn_in, w_ffn_out):
    B, L, _ = x.shape
    assert B == 1 and L % IN_ROWS == 0 and L % KEY_GROUP == 0
    depth = w_in.shape[0]

    grp = jnp.asarray(_group_matrix(SA_WIDTH, SA_DH), bf16)
    expm = jnp.asarray(_iw_expand(IDX_DH), bf16)
    slt = jnp.asarray(_strict_lower(KEY_TILE), bf16)
    ms_np, lvl_np = _hgrn_constants()
    mstack = jnp.asarray(ms_np, bf16)
    lvl = jnp.asarray(lvl_np, i32)

    x2d = x[0]
    mem2d = mem[0]
    ik0 = IN_MAIN - MEM_WIDTH
    for layer in range(depth):
        w = w_in[layer]
        w_perm = jnp.concatenate(
            [w[:, :ik0], w[:, ik0 + IDX_DH + IDX_HEADS:], w[:, ik0:ik0 + IDX_DH + IDX_HEADS],
             jnp.zeros((D_MODEL, IN_PAD - w.shape[1]), w.dtype)], axis=1).astype(bf16)
        tile4 = lambda g: jnp.tile(g[layer], 4)[None, :]
        km, vmT = _memkv(mem2d, norm_mem[layer][None, :], w_mem_kv[layer].astype(bf16),
                         tile4(mem_k_gain), grp)
        hgp, qT, k, vT, iqT, ik, sgT, mqT = _inproj(
            x2d, norm_mix[layer][None, :], w_perm, tile4(sa_q_gain), tile4(sa_k_gain),
            tile4(mem_q_gain), grp, expm)
        o_hg = _hgrn(layer, hgp, lb_logits, hg_out_gain[layer][None, :], mstack, lvl)
        o_sm = _dsa(qT, iqT, sgT, mqT, ik, k, vT, km, vmT, slt)
        x2d = _outffn(x2d, o_hg, o_sm, w_out[layer].astype(bf16), norm_ffn[layer][None, :],
                      w_ffn_in[layer].astype(bf16), w_ffn_out[layer].astype(bf16))
    return x2d[None]
```

```python
import functools
import math

import numpy as np
import jax
import jax.numpy as jnp
from jax import lax
from jax.experimental import pallas as pl
from jax.experimental.pallas import tpu as pltpu

f32 = jnp.float32
bf16 = jnp.bfloat16
i32 = jnp.int32

D_MODEL = 1024
HG_HEADS = 4
HG_DK = 128
HG_DV = 128
HG_WIDTH = HG_HEADS * HG_DV
SA_HEADS = 4
SA_DH = 64
SA_WIDTH = SA_HEADS * SA_DH
MEM_HEADS = 4
MEM_DH = 64
MEM_WIDTH = MEM_HEADS * MEM_DH
IDX_HEADS = 8
IDX_DH = 64
IDX_WIDTH = IDX_HEADS * IDX_DH
TOPK_MAX = 256
D_FF = 2816
EPS = 1e-6

LANES = 128
SUBLANES = 8
WORD_BITS = 32
HG_IN = 4 * HG_WIDTH
IN_MAIN = HG_IN + 3 * SA_WIDTH + IDX_WIDTH + MEM_WIDTH
IN_PAD = IN_MAIN + LANES
IW_LANE0 = IDX_DH

HG_CHUNK = 128
HG_LEVELS = 7
HG_MATMUL_BLOCKS = (2, 4)
Q_ROWS = LANES
KEY_TILE = 128
KEY_GROUP = WORD_BITS * SUBLANES
GROUP_UNROLL = 16
KEY_CHUNK = 2 * KEY_GROUP
SWEEP_KEYS = 2 * KEY_CHUNK
SUPER_KEYS = 2 * SWEEP_KEYS
IN_ROWS = 512
FFN_ROWS = 256

INT_MIN = -2 ** 31
NEG_BIG = -1e30
LOG2E = math.log2(math.e)
VMEM_LIMIT = 56 * 1024 * 1024


def _dot(a, b):
    return jnp.dot(a, b, preferred_element_type=f32)


def _dot_nt(a, b):
    return lax.dot_general(a, b, (((1,), (1,)), ((), ())), preferred_element_type=f32)


def _dot_tn(a, b):
    return lax.dot_general(a, b, (((0,), (0,)), ((), ())), preferred_element_type=f32)


def _split3(x):
    a = x.astype(bf16)
    r = x - a.astype(f32)
    b = r.astype(bf16)
    c = (r - b.astype(f32)).astype(bf16)
    return a, b, c


def _dot_sel_right(x, m01):
    a, b, c = _split3(x)
    return _dot(a, m01) + _dot(b, m01) + _dot(c, m01)


def _dot_sel_left(m01, x):
    a, b, c = _split3(x)
    return _dot(m01, a) + _dot(m01, b) + _dot(m01, c)


def _rmsnorm_rows(x, gain):
    ms = jnp.mean(x * x, axis=-1, keepdims=True)
    return x * lax.rsqrt(ms + EPS) * gain


def _headnorm(t, gain, grp, width):
    ss = _dot_sel_right(t * t, grp)
    return t * lax.rsqrt(ss * (1.0 / width) + EPS) * gain


def _resident(shape):
    zeros = (0,) * len(shape)
    return pl.BlockSpec(shape, lambda *_: zeros, pipeline_mode=pl.Buffered(1))


def _group_matrix(n, width):
    g = np.arange(n) // width
    return (g[:, None] == g[None, :]).astype(np.float32)


def _iw_expand(width):
    m = np.zeros((LANES, IDX_HEADS * width), np.float32)
    for h in range(IDX_HEADS):
        m[IW_LANE0 + h, h * width:(h + 1) * width] = 1.0
    return m


def _strict_lower(n):
    a = np.arange(n)
    return (a[:, None] > a[None, :]).astype(np.float32)


def _hgrn_constants():
    n = HG_CHUNK
    t = np.arange(n)[:, None]
    r = np.arange(n)[None, :]
    mats = []
    for h in HG_MATMUL_BLOCKS:
        mats.append(((r >= (t // h) * h) & (r <= t)).astype(np.float32))
        mats.append(((r > t) & (r <= (t // h) * h + h - 1)).astype(np.float32))
    mats.append((r <= t).astype(np.float32))
    stack = np.concatenate(mats, axis=0)
    x = t ^ r
    lvl = np.where(x == 0, 0, np.floor(np.log2(np.maximum(x, 1))).astype(np.int64) + 1)
    lvl = np.where(r > t, -1, lvl).astype(np.int32)
    return stack, lvl


def _memkv_kernel(mem_ref, g_ref, w_ref, gk_ref, grp_ref, km_ref, vmT_ref):
    mn = _rmsnorm_rows(mem_ref[...], g_ref[...]).astype(bf16)
    kv = _dot(mn, w_ref[...])
    km_ref[...] = _headnorm(kv[:, :MEM_WIDTH], gk_ref[...], grp_ref[...], MEM_DH).astype(bf16)
    vmT_ref[...] = kv[:, MEM_WIDTH:].T.astype(bf16)


def _memkv(mem2d, g_mem, w_kv, gk, grp):
    m = mem2d.shape[0]
    return pl.pallas_call(
        _memkv_kernel,
        out_shape=(jax.ShapeDtypeStruct((m, MEM_WIDTH), bf16),
                   jax.ShapeDtypeStruct((MEM_WIDTH, m), bf16)),
        name="memkv",
    )(mem2d, g_mem, w_kv, gk, grp)


def _inproj_kernel(x_ref, g_ref, w_ref, gq_ref, gk_ref, gm_ref, grp_ref, exp_ref,
                   hgp_ref, qT_ref, k_ref, vT_ref, iqT_ref, ik_ref, sgT_ref, mqT_ref):
    h = _rmsnorm_rows(x_ref[...], g_ref[...]).astype(bf16)
    p = _dot(h, w_ref[...])
    hgp_ref[...] = p[:, :HG_IN]
    o = HG_IN
    sq = p[:, o:o + SA_WIDTH]
    sk = p[:, o + SA_WIDTH:o + 2 * SA_WIDTH]
    sv = p[:, o + 2 * SA_WIDTH:o + 3 * SA_WIDTH]
    o += 3 * SA_WIDTH
    iq = p[:, o:o + IDX_WIDTH]
    o += IDX_WIDTH
    mq = p[:, o:o + MEM_WIDTH]
    last = p[:, IN_MAIN:IN_PAD]
    grp = grp_ref[...]
    qn = _headnorm(sq, gq_ref[...], grp, SA_DH) * (SA_DH ** -0.5 * LOG2E)
    qT_ref[...] = qn.T.astype(bf16)
    k_ref[...] = _headnorm(sk, gk_ref[...], grp, SA_DH).astype(bf16)
    vT_ref[...] = sv.T.astype(bf16)
    wabs = _dot_sel_right(jnp.abs(last), exp_ref[...]) * (IDX_HEADS ** -0.5 * IDX_DH ** -0.5)
    iqT_ref[...] = (iq * wabs).T.astype(bf16)
    ik_ref[...] = last[:, :IDX_DH].astype(bf16)
    sgn = jnp.where(last >= 0.0, 1.0, -1.0).T
    sgT_ref[...] = sgn[IW_LANE0:IW_LANE0 + IDX_HEADS, :]
    mqn = _headnorm(mq, gm_ref[...], grp, MEM_DH) * (MEM_DH ** -0.5 * LOG2E)
    mqT_ref[...] = mqn.T.astype(bf16)


def _inproj(x2d, g_mix, w_in, gq, gk, gm, grp, expm):
    L = x2d.shape[0]
    tm = IN_ROWS
    rows = lambda w: pl.BlockSpec((tm, w), lambda i: (i, 0))
    cols = lambda h: pl.BlockSpec((h, tm), lambda i: (0, i))
    return pl.pallas_call(
        _inproj_kernel,
        grid=(L // tm,),
        in_specs=[rows(D_MODEL), _resident((1, D_MODEL)), _resident((D_MODEL, IN_PAD)),
                  _resident((1, SA_WIDTH)), _resident((1, SA_WIDTH)), _resident((1, MEM_WIDTH)),
                  _resident((SA_WIDTH, SA_WIDTH)), _resident((LANES, IDX_WIDTH))],
        out_specs=(rows(HG_IN), cols(SA_WIDTH), rows(SA_WIDTH), cols(SA_WIDTH),
                   cols(IDX_WIDTH), rows(IDX_DH), cols(IDX_HEADS), cols(MEM_WIDTH)),
        out_shape=(jax.ShapeDtypeStruct((L, HG_IN), f32),
                   jax.ShapeDtypeStruct((SA_WIDTH, L), bf16),
                   jax.ShapeDtypeStruct((L, SA_WIDTH), bf16),
                   jax.ShapeDtypeStruct((SA_WIDTH, L), bf16),
                   jax.ShapeDtypeStruct((IDX_WIDTH, L), bf16),
                   jax.ShapeDtypeStruct((L, IDX_DH), bf16),
                   jax.ShapeDtypeStruct((IDX_HEADS, L), f32),
                   jax.ShapeDtypeStruct((MEM_WIDTH, L), bf16)),
        compiler_params=pltpu.CompilerParams(dimension_semantics=("arbitrary",),
                                             vmem_limit_bytes=VMEM_LIMIT),
        name="inproj",
    )(x2d, g_mix, w_in, gq, gk, gm, grp, expm)


def _block_decays(b, h):
    n = b.shape[0]
    br = b.reshape(n // h, h, LANES)
    last = br[:, h - 1:h, :]
    prev = jnp.concatenate([jnp.zeros((1, 1, LANES), f32), last[:-1]], axis=0)
    return (br - prev).reshape(n, LANES), (last - br).reshape(n, LANES)


def _hgrn_head(lb, q, z, v, gate, out_gain, mstack, lvl, st):
    n = HG_CHUNK
    log_sig = jnp.minimum(z, 0.0) - jnp.log1p(jnp.exp(-jnp.abs(z)))
    c = jnp.log1p(-lb) + log_sig
    a = jnp.log(lb)
    lf = jnp.maximum(a, c) + jnp.log1p(jnp.exp(-jnp.abs(a - c)))
    kk = (1.0 - lb) * jax.nn.sigmoid(-z)
    qf = q * jax.nn.sigmoid(q)
    v = v.astype(bf16)

    ex = _dot_sel_left(mstack, lf)
    nm = len(HG_MATMUL_BLOCKS)
    eb = ex[2 * nm * n:(2 * nm + 1) * n]
    kb = kk.astype(bf16)
    att = jnp.where(lvl == 0, _dot_nt(qf.astype(bf16), kb), 0.0)
    att = jnp.where(lvl == 1, _dot_nt((qf * jnp.exp(lf)).astype(bf16), kb), att)
    for lv in range(2, HG_LEVELS + 1):
        h = 1 << (lv - 1)
        if h in HG_MATMUL_BLOCKS:
            i = HG_MATMUL_BLOCKS.index(h)
            eq, ek = ex[2 * i * n:(2 * i + 1) * n], ex[(2 * i + 1) * n:(2 * i + 2) * n]
        else:
            eq, ek = _block_decays(eb, h)
        part = _dot_nt((qf * jnp.exp(eq)).astype(bf16), (kk * jnp.exp(ek)).astype(bf16))
        att = jnp.where(lvl == lv, part, att)
    ekd = eb[n - 1:n, :] - eb
    o = _dot_nt((qf * jnp.exp(eb)).astype(bf16), st.astype(bf16)) + _dot(att.astype(bf16), v)
    st = st * jnp.exp(eb[n - 1:n, :]) + _dot_tn(v, (kk * jnp.exp(ekd)).astype(bf16))
    return _rmsnorm_rows(o, out_gain) * (gate * jax.nn.sigmoid(gate)), st


def _hgrn_kernel(layer, q_ref, f_ref, i_ref, g_ref, lbl_ref, og_ref, ms_ref, lvl_ref,
                 o_ref, st_ref):
    @pl.when(pl.program_id(0) == 0)
    def _():
        st_ref[...] = jnp.zeros_like(st_ref)

    lg = lbl_ref[...]
    e = jnp.exp(lg - jnp.max(lg, axis=0, keepdims=True))
    sm = e / jnp.sum(e, axis=0, keepdims=True)
    lb = jnp.zeros((1, HG_WIDTH), f32)
    for j in range(1, layer + 1):
        lb = lb + sm[j:j + 1, :]

    for h in range(HG_HEADS):
        sl = slice(h * HG_DK, (h + 1) * HG_DK)
        o, st = _hgrn_head(lb[:, sl], q_ref[:, sl], f_ref[:, sl], i_ref[:, sl], g_ref[:, sl],
                           og_ref[...], ms_ref[...], lvl_ref[...], st_ref[h])
        st_ref[h] = st
        o_ref[:, sl] = o.astype(bf16)


def _hgrn(layer, hgp, lb_logits, out_gain, mstack, lvl):
    L = hgp.shape[0]
    depth = lb_logits.shape[0]
    n = HG_CHUNK
    col = lambda g: pl.BlockSpec((n, HG_WIDTH), lambda t, g=g: (t, g))
    return pl.pallas_call(
        functools.partial(_hgrn_kernel, layer),
        grid=(L // n,),
        in_specs=[col(0), col(1), col(2), col(3),
                  _resident((depth, HG_WIDTH)),
                  _resident((1, HG_DV)), _resident(mstack.shape), _resident((n, n))],
        out_specs=pl.BlockSpec((n, HG_WIDTH), lambda t: (t, 0)),
        out_shape=jax.ShapeDtypeStruct((L, HG_WIDTH), bf16),
        scratch_shapes=[pltpu.VMEM((HG_HEADS, HG_DV, HG_DK), f32)],
        compiler_params=pltpu.CompilerParams(dimension_semantics=("arbitrary",),
                                             vmem_limit_bytes=VMEM_LIMIT),
        name="hgrn",
    )(hgp, hgp, hgp, hgp, lb_logits, out_gain, mstack, lvl)


def _bit_transpose32(w):
    w = list(w)
    j, m = 16, 0x0000FFFF
    while j:
        k = 0
        while k < WORD_BITS:
            t = (w[k] ^ lax.shift_right_logical(w[k + j], jnp.full_like(w[k], j))) & m
            w[k] = w[k] ^ t
            w[k + j] = w[k + j] ^ lax.shift_left(t, jnp.full_like(t, j))
            k = (k + j + 1) & ~j
        j >>= 1
        m = (m ^ (m << j)) & 0xFFFFFFFF
    return w


def _pair_blockdiag(ref, pair, width):
    z = jnp.zeros((width, LANES), bf16)
    a = ref[(2 * pair) * width:(2 * pair + 1) * width, :]
    b = ref[(2 * pair + 1) * width:(2 * pair + 2) * width, :]
    return jnp.concatenate([jnp.concatenate([a, z], axis=1), jnp.concatenate([z, b], axis=1)], axis=0)


def _dsa_kernel(topk, qT_ref, iqT_ref, sgT_ref, mqT_ref, ik_ref, k_ref, vT_ref, km_ref, vmT_ref,
                slt_ref, o_ref,
                key_scr, plane_scr, alive_scr, rhs_scr, qbd_scr, lg_scr, acc_scr):
    R, G, U, CH = Q_ROWS, KEY_GROUP, GROUP_UNROLL, KEY_CHUNK
    SCK = lg_scr.shape[1]
    blk = pl.program_id(0)
    ntile = blk + 1
    nchunk = (blk * R + R + CH - 1) // CH
    ngrp = nchunk * (CH // G)
    niter = (ngrp + U - 1) // U
    row = blk * R + lax.broadcasted_iota(i32, (KEY_TILE, R), 1)
    sub = lax.broadcasted_iota(i32, (KEY_TILE, R), 0)

    for p in range(IDX_HEADS // 2):
        rhs_scr[p] = jnp.concatenate([iqT_ref[(2 * p) * IDX_DH:(2 * p + 1) * IDX_DH, :],
                                      iqT_ref[(2 * p + 1) * IDX_DH:(2 * p + 2) * IDX_DH, :]], axis=1)
    for p in range(SA_HEADS // 2):
        qbd_scr[p] = _pair_blockdiag(qT_ref, p, SA_DH)
    sgn = sgT_ref[...]

    def score_group(g, diagonal):
        words = []
        for t in range(G // KEY_TILE):
            k0 = pl.multiple_of(g * G + t * KEY_TILE, KEY_TILE)
            ikt = ik_ref[pl.ds(k0, KEY_TILE), :]
            acc = jnp.zeros((KEY_TILE, R), f32)
            for p in range(IDX_HEADS // 2):
                s2 = _dot(ikt, rhs_scr[p])
                acc = acc + jnp.maximum(s2[:, :R], 0.0) * sgn[2 * p:2 * p + 1, :]
                acc = acc + jnp.maximum(s2[:, R:], 0.0) * sgn[2 * p + 1:2 * p + 2, :]
            bits = pltpu.bitcast(acc, i32)
            keys = bits ^ ((bits >> 31) & 0x7FFFFFFF)
            if diagonal:
                keys = jnp.where(sub + k0 <= row, keys, INT_MIN)
            key_scr[pl.ds(k0, KEY_TILE), :] = keys
            words += [keys[j * SUBLANES:(j + 1) * SUBLANES, :]
                      for j in range(KEY_TILE // SUBLANES)]
        planes = _bit_transpose32(words)
        plane_scr[g, 0] = ~planes[0]
        for b in range(1, WORD_BITS):
            plane_scr[g, b] = planes[b]
        alive_scr[g] = jnp.full((SUBLANES, R), -1, i32)

    def score_chunk(c, carry):
        for gg in range(CH // G):
            score_group(c * (CH // G) + gg, False)
        return carry

    lax.fori_loop(0, nchunk - 1, score_chunk, 0)
    for gg in range(CH // G):
        score_group((nchunk - 1) * (CH // G) + gg, True)

    def clear_group(g, carry):
        alive_scr[g] = jnp.zeros((SUBLANES, R), i32)
        for b in range(WORD_BITS):
            plane_scr[g, b] = jnp.zeros((SUBLANES, R), i32)
        return carry

    lax.fori_loop(ngrp, niter * U, clear_group, 0)

    def count_pass(p, tm_prev):
        tmb = None if tm_prev is None else jnp.broadcast_to(tm_prev, (SUBLANES, R))

        def body(it, pcs):
            pcs = list(pcs)
            for u in range(U):
                g = it * U + u
                a = alive_scr[g]
                if tmb is not None:
                    a = a & ~(plane_scr[g, p - 1] ^ tmb)
                    alive_scr[g] = a
                pcs[u % len(pcs)] = pcs[u % len(pcs)] + lax.population_count(a & plane_scr[g, p])
            return tuple(pcs)

        pcs = lax.fori_loop(0, niter, body, (jnp.zeros((SUBLANES, R), i32),) * 4)
        pc = (pcs[0] + pcs[1]) + (pcs[2] + pcs[3])
        return jnp.sum(pc.astype(f32), axis=0, keepdims=True)

    def decide(p, cnt, above, thr):
        take = above + cnt >= topk
        tm = jnp.where(take, -1, 0).astype(i32)
        thr = thr | (tm & jnp.left_shift(jnp.int32(1), WORD_BITS - 1 - p))
        above = jnp.where(take, above, above + cnt)
        return tm, above, thr

    def radix_pass(p, carry):
        tm, above, thr = carry
        return decide(p, count_pass(p, tm), above, thr)

    zero = jnp.zeros((1, R), f32)
    carry = decide(0, count_pass(0, None), zero, jnp.zeros((1, R), i32))
    tm, above, thr_u = lax.fori_loop(1, WORD_BITS, radix_pass, carry)

    tmb = jnp.broadcast_to(tm, (SUBLANES, R))

    def tie_body(it, pc):
        for u in range(U):
            g = it * U + u
            pc = pc + lax.population_count(alive_scr[g] & ~(plane_scr[g, WORD_BITS - 1] ^ tmb))
        return pc

    ties = jnp.sum(lax.fori_loop(0, niter, tie_body, jnp.zeros((SUBLANES, R), i32)).astype(f32),
                   axis=0, keepdims=True)
    need = topk - above
    thr = jnp.maximum(thr_u ^ INT_MIN, INT_MIN + 1)

    excess = jnp.max(jnp.where((ties > need) & (thr_u != 0), 1.0, 0.0)) > 0.0

    @pl.when(excess)
    def _():
        def fix_tile(t, seen):
            k0 = pl.multiple_of(t * KEY_TILE, KEY_TILE)
            keys = key_scr[pl.ds(k0, KEY_TILE), :]
            eq = jnp.where(keys == thr, 1.0, 0.0)
            before = _dot(slt_ref[...], eq.astype(bf16)) + seen
            drop = eq * jnp.where(before >= need, 1.0, 0.0)
            key_scr[pl.ds(k0, KEY_TILE), :] = jnp.where(drop > 0.0, INT_MIN, keys)
            return seen + jnp.sum(eq, axis=0, keepdims=True)

        lax.fori_loop(0, ntile, fix_tile, zero)

    acc_scr[...] = jnp.zeros_like(acc_scr)
    SW = SWEEP_KEYS
    nsweep = (nchunk * CH + SW - 1) // SW
    nper = SCK // SW

    @pl.when(nsweep * SW > nchunk * CH)
    def _():
        key_scr[pl.ds(pl.multiple_of(nchunk * CH, CH), CH), :] = jnp.full((CH, R), INT_MIN, i32)

    def fold(x, op):
        return op(x.reshape(x.shape[0] // SUBLANES, SUBLANES, R), axis=0)

    def attend_super(sc, carry):
        m_old, l_old = carry
        c_cnt = jnp.minimum(nper, nsweep - sc * nper)
        key0 = sc * SCK

        def logits_chunk(ci, mx):
            mx = list(mx)
            for gg in range(SW // G):
                off = pl.multiple_of(ci * SW + gg * G, G)
                k0 = pl.multiple_of(key0 + off, G)
                sel = key_scr[pl.ds(k0, G), :] >= thr
                kt = k_ref[pl.ds(k0, G), :]
                for p in range(SA_HEADS // 2):
                    lg2 = _dot(kt[:, p * 2 * SA_DH:(p + 1) * 2 * SA_DH], qbd_scr[p])
                    for hh in range(2):
                        h = 2 * p + hh
                        lg = jnp.where(sel, lg2[:, hh * R:(hh + 1) * R], NEG_BIG)
                        lg_scr[h, pl.ds(off, G), :] = lg
                        mx[h] = jnp.maximum(mx[h], fold(lg, jnp.max))
            return tuple(mx)

        neg = jnp.full((SUBLANES, R), NEG_BIG, f32)
        mx = lax.fori_loop(0, c_cnt, logits_chunk, (neg,) * SA_HEADS)
        m_new = jnp.maximum(m_old, jnp.concatenate(
            [jnp.max(t, axis=0, keepdims=True) for t in mx], axis=0))
        alpha = jnp.exp2(m_old - m_new)

        def exp_chunk(ci, carry):
            ls, pv = list(carry[0]), list(carry[1])
            off = pl.multiple_of(ci * SW, SW)
            k0 = pl.multiple_of(key0 + off, SW)
            for gg in range(SW // G):
                og = pl.multiple_of(off + gg * G, G)
                kg = pl.multiple_of(k0 + gg * G, G)
                for h in range(SA_HEADS):
                    pr = jnp.exp2(lg_scr[h, pl.ds(og, G), :] - m_new[h:h + 1, :])
                    ls[h] = ls[h] + fold(pr, jnp.sum)
                    pv[h] = pv[h] + _dot(vT_ref[h * SA_DH:(h + 1) * SA_DH, pl.ds(kg, G)],
                                         pr.astype(bf16))
            return tuple(ls), tuple(pv)

        ls, pv = lax.fori_loop(0, c_cnt, exp_chunk,
                               ((jnp.zeros((SUBLANES, R), f32),) * SA_HEADS,
                                (jnp.zeros((SA_DH, R), f32),) * SA_HEADS))
        for h in range(SA_HEADS):
            acc_scr[h] = alpha[h:h + 1, :] * acc_scr[h] + pv[h]
        l_new = alpha * l_old + jnp.concatenate(
            [jnp.sum(t, axis=0, keepdims=True) for t in ls], axis=0)
        return m_new, l_new

    nsuper = (nsweep + nper - 1) // nper
    _, l_fin = lax.fori_loop(0, nsuper, attend_super,
                             (jnp.full((SA_HEADS, R), NEG_BIG, f32), jnp.zeros((SA_HEADS, R), f32)))

    outs = [acc_scr[h] / l_fin[h:h + 1, :] for h in range(SA_HEADS)]

    km = km_ref[...]
    for p in range(MEM_HEADS // 2):
        lg2 = _dot(km[:, p * 2 * MEM_DH:(p + 1) * 2 * MEM_DH], _pair_blockdiag(mqT_ref, p, MEM_DH))
        for hh in range(2):
            h = 2 * p + hh
            lg = lg2[:, hh * R:(hh + 1) * R]
            pr = jnp.exp2(lg - jnp.max(lg, axis=0, keepdims=True))
            pv = _dot(vmT_ref[h * MEM_DH:(h + 1) * MEM_DH, :], pr.astype(bf16))
            outs.append(pv / jnp.sum(pr, axis=0, keepdims=True))

    o_ref[...] = jnp.concatenate(outs, axis=0).T.astype(bf16)


def _dsa(qT, iqT, sgT, mqT, ik, k, vT, km, vmT, slt):
    L = k.shape[0]
    n_mem = km.shape[0]
    topk = min(TOPK_MAX, L // 4)
    R = Q_ROWS
    ngroups = L // KEY_GROUP + GROUP_UNROLL
    sck = min(SUPER_KEYS, L)
    assert L % sck == 0 and sck % SWEEP_KEYS == 0
    cols = lambda h: pl.BlockSpec((h, R), lambda i: (0, i))
    return pl.pallas_call(
        functools.partial(_dsa_kernel, topk),
        grid=(L // R,),
        in_specs=[cols(SA_WIDTH), cols(IDX_WIDTH), cols(IDX_HEADS), cols(MEM_WIDTH),
                  _resident((L, IDX_DH)), _resident((L, SA_WIDTH)), _resident((SA_WIDTH, L)),
                  _resident((n_mem, MEM_WIDTH)), _resident((MEM_WIDTH, n_mem)),
                  _resident((KEY_TILE, KEY_TILE))],
        out_specs=pl.BlockSpec((R, SA_WIDTH + MEM_WIDTH), lambda i: (i, 0)),
        out_shape=jax.ShapeDtypeStruct((L, SA_WIDTH + MEM_WIDTH), bf16),
        scratch_shapes=[pltpu.VMEM((L, R), i32),
                        pltpu.VMEM((ngroups, WORD_BITS, SUBLANES, R), i32),
                        pltpu.VMEM((ngroups, SUBLANES, R), i32),
                        pltpu.VMEM((IDX_HEADS // 2, IDX_DH, 2 * R), bf16),
                        pltpu.VMEM((SA_HEADS // 2, 2 * SA_DH, 2 * R), bf16),
                        pltpu.VMEM((SA_HEADS, sck, R), f32),
                        pltpu.VMEM((SA_HEADS, SA_DH, R), f32)],
        compiler_params=pltpu.CompilerParams(dimension_semantics=("arbitrary",),
                                             vmem_limit_bytes=VMEM_LIMIT),
        name="dsa",
    )(qT, iqT, sgT, mqT, ik, k, vT, km, vmT, slt)


def _outffn_kernel(x_ref, ohg_ref, osm_ref, wo_ref, g_ref, wi_ref, wf_ref, o_ref):
    x1 = (x_ref[...] + _dot(ohg_ref[...], wo_ref[:HG_WIDTH, :])
          + _dot(osm_ref[...], wo_ref[HG_WIDTH:, :]))
    h = _rmsnorm_rows(x1, g_ref[...]).astype(bf16)
    a = _dot(h, wi_ref[:, :D_FF])
    b = _dot(h, wi_ref[:, D_FF:])
    y = (a * jax.nn.sigmoid(a) * b).astype(bf16)
    o_ref[...] = x1 + _dot(y, wf_ref[...])


def _outffn(x2d, o_hg, o_sm, w_out, g_ffn, w_ffn_in, w_ffn_out):
    L = x2d.shape[0]
    tm = FFN_ROWS
    rows = lambda w: pl.BlockSpec((tm, w), lambda i: (i, 0))
    return pl.pallas_call(
        _outffn_kernel,
        grid=(L // tm,),
        in_specs=[rows(D_MODEL), rows(HG_WIDTH), rows(SA_WIDTH + MEM_WIDTH),
                  _resident((D_MODEL, D_MODEL)), _resident((1, D_MODEL)),
                  _resident((D_MODEL, 2 * D_FF)), _resident((D_FF, D_MODEL))],
        out_specs=rows(D_MODEL),
        out_shape=jax.ShapeDtypeStruct((L, D_MODEL), f32),
        compiler_params=pltpu.CompilerParams(dimension_semantics=("arbitrary",),
                                             vmem_limit_bytes=VMEM_LIMIT),
        name="outffn",
    )(x2d, o_hg, o_sm, w_out, g_ffn, w_ffn_in, w_ffn_out)


def kernel(x, mem, w_in, w_out, w_mem_kv, lb_logits, norm_mix, norm_mem, norm_ffn, hg_out_gain,
           sa_q_gain, sa_k_gain, mem_q_gain, mem_k_gain, w_ffn_in, w_ffn_out):
    B, L, _ = x.shape
    assert B == 1 and L % IN_ROWS == 0 and L % KEY_GROUP == 0
    depth = w_in.shape[0]

    grp = jnp.asarray(_group_matrix(SA_WIDTH, SA_DH), bf16)
    expm = jnp.asarray(_iw_expand(IDX_DH), bf16)
    slt = jnp.asarray(_strict_lower(KEY_TILE), bf16)
    ms_np, lvl_np = _hgrn_constants()
    mstack = jnp.asarray(ms_np, bf16)
    lvl = jnp.asarray(lvl_np, i32)

    x2d = x[0]
    mem2d = mem[0]
    ik0 = IN_MAIN - MEM_WIDTH
    for layer in range(depth):
        w = w_in[layer]
        w_perm = jnp.concatenate(
            [w[:, :ik0], w[:, ik0 + IDX_DH + IDX_HEADS:], w[:, ik0:ik0 + IDX_DH + IDX_HEADS],
             jnp.zeros((D_MODEL, IN_PAD - w.shape[1]), w.dtype)], axis=1).astype(bf16)
        tile4 = lambda g: jnp.tile(g[layer], 4)[None, :]
        km, vmT = _memkv(mem2d, norm_mem[layer][None, :], w_mem_kv[layer].astype(bf16),
                         tile4(mem_k_gain), grp)
        hgp, qT, k, vT, iqT, ik, sgT, mqT = _inproj(
            x2d, norm_mix[layer][None, :], w_perm, tile4(sa_q_gain), tile4(sa_k_gain),
            tile4(mem_q_gain), grp, expm)
        o_hg = _hgrn(layer, hgp, lb_logits, hg_out_gain[layer][None, :], mstack, lvl)
        o_sm = _dsa(qT, iqT, sgT, mqT, ik, k, vT, km, vmT, slt)
        x2d = _outffn(x2d, o_hg, o_sm, w_out[layer].astype(bf16), norm_ffn[layer][None, :],
                      w_ffn_in[layer].astype(bf16), w_ffn_out[layer].astype(bf16))
    return x2d[None]
```
